```python
import math
import jax, jax.numpy as jnp
from jax import lax
import numpy as np

D_MODEL = 2048
BATCH = 4
SEQ = 2048
DEPTH = 1

MEM_LEN = 256
D_RNN = 1024
RNN_BLOCKS = 8
RNN_BLOCK = D_RNN // RNN_BLOCKS
CONV_W = 4
LRU_C = 8.0
SWA_HEADS = 16
SWA_KV_HEADS = 2
SWA_GROUP = SWA_HEADS // SWA_KV_HEADS
SWA_HD = 64
D_SWA = SWA_HEADS * SWA_HD
D_SWA_KV = SWA_KV_HEADS * SWA_HD
WINDOW = 128
BLOCK = WINDOW
MEM_HEADS = 4
MEM_HD = 256
D_MEM = MEM_HEADS * MEM_HD
REL_BUCKETS = 32
REL_MAX_DIST = 128
N_BRANCH = 3
EPS = 1e-6
NEG_INF = -1e30

IN_SPLITS = (D_RNN, D_RNN, D_SWA, D_SWA_KV, D_SWA_KV, D_SWA, D_MEM, D_MEM, N_BRANCH * D_MODEL)
D_IN = 2 * D_RNN + 2 * D_SWA + 2 * D_SWA_KV + 2 * D_MEM + N_BRANCH * D_MODEL

kernel_name = "hybrid_rglru_swa_sink_memxattn_gated"


def rmsnorm(x, g):
    xf = x.astype(jnp.float32)
    y = xf * lax.rsqrt(jnp.mean(xf * xf, axis=-1, keepdims=True) + EPS)
    return (y * g.astype(jnp.float32)).astype(x.dtype)


def rel_bucket(dist):
    n = jnp.maximum(dist, 0)
    max_exact = REL_BUCKETS // 2
    ratio = jnp.log(jnp.maximum(n, 1).astype(jnp.float32) / max_exact) / math.log(REL_MAX_DIST / max_exact)
    large = max_exact + (ratio * (REL_BUCKETS - max_exact)).astype(jnp.int32)
    large = jnp.minimum(large, REL_BUCKETS - 1)
    return jnp.where(n < max_exact, n, large)


def rglru_branch(xr, conv_w, conv_b, w_a, b_a, w_x, b_x, lam):
    B, S, _ = xr.shape
    xp = jnp.pad(xr, ((0, 0), (CONV_W - 1, 0), (0, 0)))
    conv = conv_b + sum(xp[:, CONV_W - 1 - k: CONV_W - 1 - k + S] * conv_w[k] for k in range(CONV_W))
    cb = conv.reshape(B, S, RNN_BLOCKS, RNN_BLOCK)
    gate_r = jax.nn.sigmoid((jnp.einsum('bsni,nij->bsnj', cb, w_a).reshape(B, S, D_RNN) + b_a).astype(jnp.float32))
    gate_i = jax.nn.sigmoid((jnp.einsum('bsni,nij->bsnj', cb, w_x).reshape(B, S, D_RNN) + b_x).astype(jnp.float32))
    log_a = -LRU_C * gate_r * jax.nn.softplus(-lam.astype(jnp.float32))
    a = jnp.exp(log_a)
    mult = jnp.sqrt(-jnp.expm1(2.0 * log_a))
    is_start = (jnp.arange(S) == 0)[None, :, None]
    mult = jnp.where(is_start, 1.0, mult)
    b = mult * gate_i * conv.astype(jnp.float32)

    def combine(e1, e2):
        a1, b1 = e1
        a2, b2 = e2
        return a1 * a2, a2 * b1 + b2

    _, h = lax.associative_scan(combine, (a, b), axis=1)
    return h.astype(xr.dtype)


def swa_branch(q, k, v, sinks, rel_bias):
    B, S, _ = q.shape
    nb = S // BLOCK
    q = q.reshape(B, nb, BLOCK, SWA_KV_HEADS, SWA_GROUP, SWA_HD)
    k = k.reshape(B, nb, BLOCK, SWA_KV_HEADS, SWA_HD)
    v = v.reshape(B, nb, BLOCK, SWA_KV_HEADS, SWA_HD)

    def with_prev(t):
        prev = jnp.concatenate([jnp.zeros_like(t[:, :1]), t[:, :-1]], axis=1)
        return jnp.concatenate([prev, t], axis=2)

    kk = with_prev(k)
    vv = with_prev(v)
    logits = jnp.einsum('bnqhgd,bnkhd->bnhgqk', q, kk).astype(jnp.float32) * (SWA_HD ** -0.5)

    qi = jnp.arange(BLOCK)[:, None]
    kj = jnp.arange(2 * BLOCK)[None, :]
    dist = qi + BLOCK - kj
    in_window = (dist >= 0) & (dist < WINDOW)
    key_abs = jnp.arange(nb)[:, None, None] * BLOCK + kj[None] - BLOCK
    valid = in_window[None] & (key_abs >= 0)

    bias = rel_bias.astype(jnp.float32)[rel_bucket(dist)]
    bias = jnp.transpose(bias, (2, 0, 1)).reshape(SWA_KV_HEADS, SWA_GROUP, BLOCK, 2 * BLOCK)
    logits = logits + bias[None, None]
    logits = jnp.where(valid[None, :, None, None], logits, NEG_INF)

    sink = sinks.astype(jnp.float32).reshape(SWA_KV_HEADS, SWA_GROUP)[None, None, :, :, None, None]
    m = jnp.maximum(jnp.max(logits, axis=-1, keepdims=True), sink)
    p = jnp.exp(logits - m)
    denom = jnp.sum(p, axis=-1, keepdims=True) + jnp.exp(sink - m)
    probs = (p / denom).astype(v.dtype)
    out = jnp.einsum('bnhgqk,bnkhd->bnqhgd', probs, vv)
    return out.reshape(B, S, D_SWA)


def mem_branch(q, mk, mv):
    B, S, _ = q.shape
    M = mk.shape[1]
    q = q.reshape(B, S, MEM_HEADS, MEM_HD)
    mk = mk.reshape(B, M, MEM_HEADS, MEM_HD)
    mv = mv.reshape(B, M, MEM_HEADS, MEM_HD)
    logits = jnp.einsum('bshd,bmhd->bhsm', q, mk).astype(jnp.float32) * (MEM_HD ** -0.5)
    probs = jax.nn.softmax(logits, axis=-1).astype(mv.dtype)
    out = jnp.einsum('bhsm,bmhd->bshd', probs, mv)
    return out.reshape(B, S, D_MEM)


def setup_inputs(seed: int = 0) -> dict:
    key = jax.random.key(seed)
    ks = jax.random.split(key, 24)
    f32 = jnp.float32
    nrm = lambda k, shape, s: jax.random.normal(k, shape, f32) * s
    L = DEPTH
    u = jax.random.uniform(ks[12], (L, D_RNN), f32, 0.9, 0.999)
    a0 = u ** (1.0 / LRU_C)
    lru_lambda = jnp.log(a0) - jnp.log1p(-a0)
    return {
        "x": nrm(ks[0], (BATCH, SEQ, D_MODEL), 1.0),
        "mem": nrm(ks[1], (BATCH, MEM_LEN, D_MODEL), 1.0),
        "pre_norm_g": 1.0 + nrm(ks[2], (L, D_MODEL), 0.02),
        "post_norm_g": 1.0 + nrm(ks[3], (L, D_MODEL), 0.02),
        "mem_norm_g": 1.0 + nrm(ks[4], (L, D_MODEL), 0.02),
        "w_in": nrm(ks[5], (L, D_MODEL, D_IN), D_MODEL ** -0.5),
        "conv_w": nrm(ks[6], (L, CONV_W, D_RNN), CONV_W ** -0.5),
        "conv_b": nrm(ks[7], (L, D_RNN), 0.02),
        "w_rg_a": nrm(ks[8], (L, RNN_BLOCKS, RNN_BLOCK, RNN_BLOCK), RNN_BLOCK ** -0.5),
        "b_rg_a": nrm(ks[9], (L, D_RNN), 0.02),
        "w_rg_x": nrm(ks[10], (L, RNN_BLOCKS, RNN_BLOCK, RNN_BLOCK), RNN_BLOCK ** -0.5),
        "b_rg_x": nrm(ks[11], (L, D_RNN), 0.02),
        "lru_lambda": lru_lambda,
        "swa_sinks": nrm(ks[13], (L, SWA_HEADS), 0.5),
        "rel_bias": nrm(ks[14], (REL_BUCKETS, SWA_HEADS), 0.5),
        "w_mem_kv": nrm(ks[15], (L, D_MODEL, 2 * D_MEM), D_MODEL ** -0.5),
        "w_br_rg": nrm(ks[16], (L, D_RNN, D_MODEL), D_RNN ** -0.5),
        "w_br_swa": nrm(ks[17], (L, D_SWA, D_MODEL), D_SWA ** -0.5),
        "w_br_mem": nrm(ks[18], (L, D_MEM, D_MODEL), D_MEM ** -0.5),
        "w_out": nrm(ks[19], (L, D_MODEL, D_MODEL), D_MODEL ** -0.5),
    }


def reference(x, mem, pre_norm_g, post_norm_g, mem_norm_g, w_in, conv_w, conv_b, w_rg_a, b_rg_a,
              w_rg_x, b_rg_x, lru_lambda, swa_sinks, rel_bias, w_mem_kv, w_br_rg, w_br_swa,
              w_br_mem, w_out):
    B, S, D = x.shape
    split_at = np.cumsum(IN_SPLITS)[:-1].tolist()
    for l in range(DEPTH):
        h = rmsnorm(x, pre_norm_g[l])
        proj = jnp.einsum('bsd,de->bse', h, w_in[l])
        (xr, g_rg, q_s, k_s, v_s, g_swa, q_m, g_mem, gate_logits) = jnp.split(proj, split_at, axis=-1)

        y_rg = rglru_branch(xr, conv_w[l], conv_b[l], w_rg_a[l], b_rg_a[l], w_rg_x[l], b_rg_x[l],
                            lru_lambda[l]) * jax.nn.silu(g_rg)
        y_swa = swa_branch(q_s, k_s, v_s, swa_sinks[l], rel_bias) * jax.nn.silu(g_swa)
        memn = rmsnorm(mem, mem_norm_g[l])
        mkv = jnp.einsum('bmd,de->bme', memn, w_mem_kv[l])
        mk, mv = jnp.split(mkv, 2, axis=-1)
        y_mem = mem_branch(q_m, mk, mv) * jax.nn.silu(g_mem)

        gates = jax.nn.sigmoid(gate_logits.astype(jnp.float32)).astype(x.dtype).reshape(B, S, N_BRANCH, D)
        merged = (gates[:, :, 0] * jnp.einsum('bsr,rd->bsd', y_rg, w_br_rg[l])
                  + gates[:, :, 1] * jnp.einsum('bsr,rd->bsd', y_swa, w_br_swa[l])
                  + gates[:, :, 2] * jnp.einsum('bsr,rd->bsd', y_mem, w_br_mem[l]))
        out = jnp.einsum('bsd,de->bse', merged, w_out[l])
        x = x + rmsnorm(out, post_norm_g[l])
    return x
```

```python
import functools
import math

import numpy as np
import jax
import jax.numpy as jnp
from jax import lax
from jax.experimental import pallas as pl
from jax.experimental.pallas import tpu as pltpu

D_MODEL = 2048
D_RNN = 1024
RNN_BLOCKS = 8
RNN_BLOCK = D_RNN // RNN_BLOCKS
CONV_W = 4
LRU_C = 8.0
SWA_HEADS = 16
SWA_KV_HEADS = 2
SWA_HD = 64
D_SWA = SWA_HEADS * SWA_HD
D_SWA_KV = SWA_KV_HEADS * SWA_HD
WINDOW = 128
MEM_HEADS = 4
MEM_HD = 256
D_MEM = MEM_HEADS * MEM_HD
REL_BUCKETS = 32
REL_MAX_DIST = 128
N_BRANCH = 3
EPS = 1e-6
NEG_INF = -1e30

_VMEM_LIMIT_V7X = 56 * 1024 * 1024
_SUBLANES = 8
_LANES = 128

_COL_XR, _COL_GRG, _COL_QS, _COL_GSWA, _COL_QM, _COL_GMEM = 0, 1, 2, 3, 4, 5
_COL_K = 6 * 1024 // D_SWA_KV
_COL_V = _COL_K + 1
D_MAIN = 6 * 1024 + 2 * D_SWA_KV

BF16 = jnp.bfloat16
F32 = jnp.float32


def _params(n_grid_dims):
    return pltpu.CompilerParams(
        dimension_semantics=("arbitrary",) * n_grid_dims,
        vmem_limit_bytes=_VMEM_LIMIT_V7X,
    )


def _const_spec(shape):
    nd = len(shape)
    return pl.BlockSpec(shape, lambda *_: (0,) * nd, pipeline_mode=pl.Buffered(1))


def _rmsnorm_rows(x, g):
    ms = jnp.mean(x * x, axis=-1, keepdims=True)
    return x * lax.rsqrt(ms + EPS) * g


def _silu(x):
    return x * jax.nn.sigmoid(x)


def _proj_kernel(x_ref, g_ref, w_ref, o_ref, h_scr, *, row_chunk, sigmoid_out):
    tm = x_ref.shape[0]

    @pl.when(pl.program_id(1) == 0)
    def _():
        for r in range(0, tm, row_chunk):
            h_scr[r:r + row_chunk, :] = _rmsnorm_rows(
                x_ref[r:r + row_chunk, :], g_ref[...]).astype(BF16)

    acc = jnp.dot(h_scr[...], w_ref[...], preferred_element_type=F32)
    o_ref[...] = jax.nn.sigmoid(acc) if sigmoid_out else acc


def _proj(x2d, g, w, *, tm, tn, sigmoid_out, name):
    m, d = x2d.shape
    n = w.shape[1]
    kern = functools.partial(_proj_kernel, row_chunk=256, sigmoid_out=sigmoid_out)
    return pl.pallas_call(
        kern,
        out_shape=jax.ShapeDtypeStruct((m, n), F32),
        grid=(m // tm, n // tn),
        in_specs=[
            pl.BlockSpec((tm, d), lambda i, j: (i, 0)),
            _const_spec((1, d)),
            pl.BlockSpec((d, tn), lambda i, j: (0, j)),
        ],
        out_specs=pl.BlockSpec((tm, tn), lambda i, j: (i, j)),
        scratch_shapes=[pltpu.VMEM((tm, d), BF16)],
        compiler_params=_params(2),
        name=name,
    )(x2d, g, w)


def _rglru_kernel(xr_ref, g_ref, cw_ref, cb_ref, wax_ref, ba_ref, bx_ref, lam_ref, y_ref,
                  xpad_scr, a_scr, b_scr, h_scr, carry_scr):
    ts, c = xr_ref.shape
    groups = ts // _SUBLANES
    s = pl.program_id(1)

    @pl.when(s == 0)
    def _():
        xpad_scr[0:_SUBLANES, :] = jnp.zeros((_SUBLANES, c), F32)
        carry_scr[...] = jnp.zeros((_SUBLANES, c), F32)

    xpad_scr[_SUBLANES:_SUBLANES + ts, :] = xr_ref[...]
    conv = cb_ref[...] + sum(
        xpad_scr[_SUBLANES - k:_SUBLANES - k + ts, :] * cw_ref[k:k + 1, :] for k in range(CONV_W))
    xpad_scr[0:_SUBLANES, :] = xpad_scr[ts:ts + _SUBLANES, :]

    conv_bf = conv.astype(BF16)
    pre = [jnp.dot(conv_bf[:, n * RNN_BLOCK:(n + 1) * RNN_BLOCK], wax_ref[n],
                   preferred_element_type=F32) for n in range(RNN_BLOCKS)]
    pre_a = jnp.concatenate([p[:, :RNN_BLOCK] for p in pre], axis=1)
    pre_x = jnp.concatenate([p[:, RNN_BLOCK:] for p in pre], axis=1)
    gate_r = jax.nn.sigmoid(pre_a + ba_ref[...])
    gate_i = jax.nn.sigmoid(pre_x + bx_ref[...])

    neg_lam = -lam_ref[...]
    softplus = jnp.maximum(neg_lam, 0.0) + jnp.log1p(jnp.exp(-jnp.abs(neg_lam)))
    log_a = -LRU_C * gate_r * softplus
    a = jnp.exp(log_a)
    th = jnp.tanh(log_a)
    mult = jnp.sqrt(-2.0 * th / (1.0 - th))
    row = lax.broadcasted_iota(jnp.int32, (ts, c), 0)
    mult = jnp.where(row == jnp.where(s == 0, 0, -1), 1.0, mult)
    b = mult * gate_i * conv

    a3 = a.reshape(groups, _SUBLANES, c)
    b3 = b.reshape(groups, _SUBLANES, c)
    sub = lax.broadcasted_iota(jnp.int32, (groups, _SUBLANES, c), 1)
    d = 1
    while d < _SUBLANES:
        a_sh = pltpu.roll(a3, d, axis=1)
        b_sh = pltpu.roll(b3, d, axis=1)
        keep = sub >= d
        b3 = jnp.where(keep, a3 * b_sh + b3, b3)
        a3 = jnp.where(keep, a3 * a_sh, a3)
        d *= 2
    a_scr[...] = a3
    b_scr[...] = b3

    def body(gi, hprev):
        hg = a_scr[gi] * hprev + b_scr[gi]
        h_scr[gi] = hg
        return jnp.broadcast_to(hg[_SUBLANES - 1:_SUBLANES, :], (_SUBLANES, c))

    carry_scr[...] = lax.fori_loop(0, groups, body, carry_scr[...])

    h = h_scr[...].reshape(ts, c)
    y_ref[...] = (h * _silu(g_ref[...])).astype(BF16)


def _rglru(pm, conv_w, conv_b, wax, b_a, b_x, lam, *, batch, seq, ts):
    steps = seq // ts
    c = D_RNN
    groups = ts // _SUBLANES
    return pl.pallas_call(
        _rglru_kernel,
        out_shape=jax.ShapeDtypeStruct((batch * seq, c), BF16),
        grid=(batch, steps),
        in_specs=[
            pl.BlockSpec((ts, c), lambda b, s: (b * steps + s, _COL_XR)),
            pl.BlockSpec((ts, c), lambda b, s: (b * steps + s, _COL_GRG)),
            _const_spec((CONV_W, c)),
            _const_spec((1, c)),
            _const_spec((RNN_BLOCKS, RNN_BLOCK, 2 * RNN_BLOCK)),
            _const_spec((1, c)),
            _const_spec((1, c)),
            _const_spec((1, c)),
        ],
        out_specs=pl.BlockSpec((ts, c), lambda b, s: (b * steps + s, 0)),
        scratch_shapes=[
            pltpu.VMEM((ts + _SUBLANES, c), F32),
            pltpu.VMEM((groups, _SUBLANES, c), F32),
            pltpu.VMEM((groups, _SUBLANES, c), F32),
            pltpu.VMEM((groups, _SUBLANES, c), F32),
            pltpu.VMEM((_SUBLANES, c), F32),
        ],
        compiler_params=_params(2),
        name="rglru",
    )(pm, pm, conv_w, conv_b, wax, b_a, b_x, lam)


def _rel_bucket_table():
    qi = np.arange(WINDOW)[:, None]
    kj = np.arange(2 * WINDOW)[None, :]
    n = np.maximum(qi + WINDOW - kj, 0)
    max_exact = REL_BUCKETS // 2
    ratio = (np.log(np.maximum(n, 1).astype(np.float32) / np.float32(max_exact))
             / np.float32(math.log(REL_MAX_DIST / max_exact))).astype(np.float32)
    large = max_exact + (ratio * np.float32(REL_BUCKETS - max_exact)).astype(np.int32)
    large = np.minimum(large, REL_BUCKETS - 1)
    return np.where(n < max_exact, n, large).astype(np.int32)


def _bias_kernel(rb_ref, bucket_ref, o_ref):
    h = pl.program_id(0)
    bucket = bucket_ref[...]
    acc = jnp.zeros(bucket.shape, F32)
    for bkt in range(REL_BUCKETS):
        acc = jnp.where(bucket == bkt, rb_ref[bkt, h], acc)
    o_ref[0] = acc


def _bias_table(rel_bias):
    bucket = jnp.asarray(_rel_bucket_table())
    return pl.pallas_call(
        _bias_kernel,
        out_shape=jax.ShapeDtypeStruct((SWA_HEADS, WINDOW, 2 * WINDOW), F32),
        grid=(SWA_HEADS,),
        in_specs=[
            pl.BlockSpec(memory_space=pltpu.SMEM),
            _const_spec((WINDOW, 2 * WINDOW)),
        ],
        out_specs=pl.BlockSpec((1, WINDOW, 2 * WINDOW), lambda h: (h, 0, 0)),
        compiler_params=_params(1),
        name="swa_bias",
    )(rel_bias, bucket)


def _swa_kernel(sink_ref, q_ref, g_ref, kc_ref, kp_ref, vc_ref, vp_ref, bias_ref, y_ref):
    blk = WINDOW
    n = pl.program_id(1)
    half = SWA_HD

    kt = jnp.concatenate([kp_ref[...], kc_ref[...]], axis=0).T
    kt_sw = jnp.concatenate([kt[half:], kt[:half]], axis=0)
    row_lo = lax.broadcasted_iota(jnp.int32, kt.shape, 0) < half
    zero_k = jnp.zeros_like(kt)
    kt_even = (jnp.where(row_lo, kt, zero_k).astype(BF16), jnp.where(row_lo, kt_sw, zero_k).astype(BF16))
    kt_odd = (jnp.where(row_lo, zero_k, kt_sw).astype(BF16), jnp.where(row_lo, zero_k, kt).astype(BF16))

    v = jnp.concatenate([vp_ref[...], vc_ref[...]], axis=0)
    v_sw = pltpu.roll(v, half, axis=1)
    lane_lo = lax.broadcasted_iota(jnp.int32, v.shape, 1) < half
    zero_v = jnp.zeros_like(v)
    v_even = (jnp.where(lane_lo, v, zero_v).astype(BF16), jnp.where(lane_lo, v_sw, zero_v).astype(BF16))
    v_odd = (jnp.where(lane_lo, zero_v, v_sw).astype(BF16), jnp.where(lane_lo, zero_v, v).astype(BF16))

    qi = lax.broadcasted_iota(jnp.int32, (blk, 2 * blk), 0)
    kj = lax.broadcasted_iota(jnp.int32, (blk, 2 * blk), 1)
    first_key = jnp.where(n > 0, 0, blk)
    valid = (kj > qi) & (kj <= qi + blk) & (kj >= first_key)
    out_lane_lo = lax.broadcasted_iota(jnp.int32, (blk, 2 * half), 1) < half

    def head_probs(logits, head):
        sc = logits * (SWA_HD ** -0.5) + bias_ref[head]
        sc = jnp.where(valid, sc, NEG_INF)
        sink = sink_ref[head]
        m = jnp.maximum(jnp.max(sc, axis=-1, keepdims=True), sink)
        p = jnp.exp(sc - m)
        denom = jnp.sum(p, axis=-1, keepdims=True) + jnp.exp(sink - m)
        return p.astype(BF16), 1.0 / denom

    pairs_per_kv = SWA_HEADS // SWA_KV_HEADS // 2
    for kv in range(SWA_KV_HEADS):
        for j in range(pairs_per_kv):
            pair = kv * pairs_per_kv + j
            cols = slice(pair * 2 * half, (pair + 1) * 2 * half)
            qp = q_ref[:, cols].astype(BF16)
            p_e, inv_e = head_probs(jnp.dot(qp, kt_even[kv], preferred_element_type=F32), 2 * pair)
            p_o, inv_o = head_probs(jnp.dot(qp, kt_odd[kv], preferred_element_type=F32), 2 * pair + 1)
            out = (jnp.dot(p_e, v_even[kv], preferred_element_type=F32)
                   + jnp.dot(p_o, v_odd[kv], preferred_element_type=F32))
            out = out * jnp.where(out_lane_lo, inv_e, inv_o)
            y_ref[:, cols] = (out * _silu(g_ref[:, cols])).astype(BF16)


def _swa(pm, sinks, bias, *, batch, seq):
    nb = seq // WINDOW
    blk = WINDOW

    def cur(col):
        return lambda b, n: (b * nb + n, col)

    def prev(col):
        return lambda b, n: (b * nb + jnp.maximum(n - 1, 0), col)

    return pl.pallas_call(
        _swa_kernel,
        out_shape=jax.ShapeDtypeStruct((batch * seq, D_SWA), BF16),
        grid=(batch, nb),
        in_specs=[
            pl.BlockSpec(memory_space=pltpu.SMEM),
            pl.BlockSpec((blk, D_SWA), cur(_COL_QS)),
            pl.BlockSpec((blk, D_SWA), cur(_COL_GSWA)),
            pl.BlockSpec((blk, D_SWA_KV), cur(_COL_K)),
            pl.BlockSpec((blk, D_SWA_KV), prev(_COL_K)),
            pl.BlockSpec((blk, D_SWA_KV), cur(_COL_V)),
            pl.BlockSpec((blk, D_SWA_KV), prev(_COL_V)),
            _const_spec((SWA_HEADS, blk, 2 * blk)),
        ],
        out_specs=pl.BlockSpec((blk, D_SWA), lambda b, n: (b * nb + n, 0)),
        compiler_params=_params(2),
        name="swa",
    )(sinks, pm, pm, pm, pm, pm, pm, bias)


def _memkv_kernel(mem_ref, g_ref, w_ref, kt_ref, v_ref):
    memn = _rmsnorm_rows(mem_ref[...], g_ref[...]).astype(BF16)
    mkv = jnp.dot(memn, w_ref[...], preferred_element_type=F32)
    kt_ref[0] = mkv[:, :D_MEM].T.astype(BF16)
    v_ref[0] = mkv[:, D_MEM:].astype(BF16)


def _memkv(mem2d, g, w, *, batch, mem_len):
    return pl.pallas_call(
        _memkv_kernel,
        out_shape=(jax.ShapeDtypeStruct((batch, D_MEM, mem_len), BF16),
                   jax.ShapeDtypeStruct((batch, mem_len, D_MEM), BF16)),
        grid=(batch,),
        in_specs=[
            pl.BlockSpec((mem_len, D_MODEL), lambda b: (b, 0)),
            _const_spec((1, D_MODEL)),
            _const_spec((D_MODEL, 2 * D_MEM)),
        ],
        out_specs=(pl.BlockSpec((1, D_MEM, mem_len), lambda b: (b, 0, 0)),
                   pl.BlockSpec((1, mem_len, D_MEM), lambda b: (b, 0, 0))),
        compiler_params=_params(1),
        name="memkv",
    )(mem2d, g, w)


def _memattn_kernel(q_ref, g_ref, kt_ref, v_ref, y_ref):
    for h in range(MEM_HEADS):
        cols = slice(h * MEM_HD, (h + 1) * MEM_HD)
        qh = q_ref[:, cols].astype(BF16)
        sc = jnp.dot(qh, kt_ref[0, cols, :], preferred_element_type=F32) * (MEM_HD ** -0.5)
        m = jnp.max(sc, axis=-1, keepdims=True)
        p = jnp.exp(sc - m)
        inv = 1.0 / jnp.sum(p, axis=-1, keepdims=True)
        out = jnp.dot(p.astype(BF16), v_ref[0, :, cols], preferred_element_type=F32) * inv
        y_ref[:, cols] = (out * _silu(g_ref[:, cols])).astype(BF16)


def _memattn(pm, kt, v, *, batch, seq, ts):
    steps = seq // ts
    mem_len = v.shape[1]
    return pl.pallas_call(
        _memattn_kernel,
        out_shape=jax.ShapeDtypeStruct((batch * seq, D_MEM), BF16),
        grid=(batch, steps),
        in_specs=[
            pl.BlockSpec((ts, D_MEM), lambda b, s: (b * steps + s, _COL_QM)),
            pl.BlockSpec((ts, D_MEM), lambda b, s: (b * steps + s, _COL_GMEM)),
            pl.BlockSpec((1, D_MEM, mem_len), lambda b, s: (b, 0, 0)),
            pl.BlockSpec((1, mem_len, D_MEM), lambda b, s: (b, 0, 0)),
        ],
        out_specs=pl.BlockSpec((ts, D_MEM), lambda b, s: (b * steps + s, 0)),
        compiler_params=_params(2),
        name="memattn",
    )(pm, pm, kt, v)


def _merge_kernel(x_ref, yr_ref, ys_ref, ym_ref, gt_ref, wbr_ref, wout_ref, g_ref, o_ref):
    d = D_MODEL
    merged = (gt_ref[:, 0:d] * jnp.dot(yr_ref[...], wbr_ref[0], preferred_element_type=F32)
              + gt_ref[:, d:2 * d] * jnp.dot(ys_ref[...], wbr_ref[1], preferred_element_type=F32)
              + gt_ref[:, 2 * d:3 * d] * jnp.dot(ym_ref[...], wbr_ref[2], preferred_element_type=F32))
    out = jnp.dot(merged.astype(BF16), wout_ref[...], preferred_element_type=F32)
    o_ref[...] = x_ref[...] + _rmsnorm_rows(out, g_ref[...])


def _merge(x2d, y_rg, y_swa, y_mem, gates, w_br, w_out, g, *, tm):
    m, d = x2d.shape
    row = lambda i: (i, 0)
    return pl.pallas_call(
        _merge_kernel,
        out_shape=jax.ShapeDtypeStruct((m, d), F32),
        grid=(m // tm,),
        in_specs=[
            pl.BlockSpec((tm, d), row),
            pl.BlockSpec((tm, D_RNN), row),
            pl.BlockSpec((tm, D_SWA), row),
            pl.BlockSpec((tm, D_MEM), row),
            pl.BlockSpec((tm, N_BRANCH * d), row),
            _const_spec((N_BRANCH, D_RNN, d)),
            _const_spec((d, d)),
            _const_spec((1, d)),
        ],
        out_specs=pl.BlockSpec((tm, d), row),
        compiler_params=_params(1),
        name="merge",
    )(x2d, y_rg, y_swa, y_mem, gates, w_br, w_out, g)


def _layer(x2d, mem2d, pre_g, post_g, mem_g, w_in, conv_w, conv_b, w_a, b_a, w_x, b_x, lam, sinks,
           bias, w_mem_kv, w_br_rg, w_br_swa, w_br_mem, w_out, *, batch, seq, mem_len):
    o = np.cumsum((D_RNN, D_RNN, D_SWA, D_SWA_KV, D_SWA_KV, D_SWA, D_MEM, D_MEM))
    cols = lambda lo, hi: w_in[:, lo:hi]
    w_main = jnp.concatenate(
        [cols(0, o[0]), cols(o[0], o[1]), cols(o[1], o[2]), cols(o[4], o[5]), cols(o[5], o[6]),
         cols(o[6], o[7]), cols(o[2], o[3]), cols(o[3], o[4])], axis=1).astype(BF16)
    w_gate = w_in[:, o[7]:].astype(BF16)
    row = lambda t: t.reshape(1, -1)

    pm = _proj(x2d, row(pre_g), w_main, tm=1024, tn=1280, sigmoid_out=False, name="proj_main")
    gates = _proj(x2d, row(pre_g), w_gate, tm=1024, tn=1024, sigmoid_out=True, name="proj_gate")

    wax = jnp.concatenate([w_a, w_x], axis=-1).astype(BF16)
    y_rg = _rglru(pm, conv_w, row(conv_b), wax, row(b_a), row(b_x), row(lam),
                  batch=batch, seq=seq, ts=512)
    y_swa = _swa(pm, sinks, bias, batch=batch, seq=seq)
    kt, v = _memkv(mem2d, row(mem_g), w_mem_kv.astype(BF16), batch=batch, mem_len=mem_len)
    y_mem = _memattn(pm, kt, v, batch=batch, seq=seq, ts=512)

    w_br = jnp.stack([w_br_rg, w_br_swa, w_br_mem]).astype(BF16)
    return _merge(x2d, y_rg, y_swa, y_mem, gates, w_br, w_out.astype(BF16), row(post_g), tm=256)


def kernel(x, mem, pre_norm_g, post_norm_g, mem_norm_g, w_in, conv_w, conv_b, w_rg_a, b_rg_a, w_rg_x,
           b_rg_x, lru_lambda, swa_sinks, rel_bias, w_mem_kv, w_br_rg, w_br_swa, w_br_mem, w_out):
    batch, seq, d = x.shape
    mem_len = mem.shape[1]
    depth = w_in.shape[0]
    bias = _bias_table(rel_bias)
    x2d = x.reshape(batch * seq, d)
    mem2d = mem.reshape(batch * mem_len, d)
    for l in range(depth):
        x2d = _layer(x2d, mem2d, pre_norm_g[l], post_norm_g[l], mem_norm_g[l], w_in[l], conv_w[l],
                     conv_b[l], w_rg_a[l], b_rg_a[l], w_rg_x[l], b_rg_x[l], lru_lambda[l],
                     swa_sinks[l], bias, w_mem_kv[l], w_br_rg[l], w_br_swa[l], w_br_mem[l], w_out[l],
                     batch=batch, seq=seq, mem_len=mem_len)
    return x2d.reshape(batch, seq, d)
```

```python
import functools
import math

import numpy as np
import jax
import jax.numpy as jnp
from jax import lax
from jax.experimental import pallas as pl
from jax.experimental.pallas import tpu as pltpu

D_MODEL = 2048
D_RNN = 1024
RNN_BLOCKS = 8
RNN_BLOCK = D_RNN // RNN_BLOCKS
CONV_W = 4
LRU_C = 8.0
SWA_HEADS = 16
SWA_KV_HEADS = 2
SWA_HD = 64
D_SWA = SWA_HEADS * SWA_HD
D_SWA_KV = SWA_KV_HEADS * SWA_HD
WINDOW = 128
MEM_HEADS = 4
MEM_HD = 256
D_MEM = MEM_HEADS * MEM_HD
REL_BUCKETS = 32
REL_MAX_DIST = 128
N_BRANCH = 3
EPS = 1e-6
NEG_INF = -1e30

_VMEM_LIMIT_V7X = 56 * 1024 * 1024
_SUBLANES = 8
_LANES = 128

_COL_XR, _COL_GRG, _COL_QS, _COL_GSWA, _COL_QM, _COL_GMEM = 0, 1, 2, 3, 4, 5
_COL_K = 6 * 1024 // D_SWA_KV
_COL_V = _COL_K + 1
D_MAIN = 6 * 1024 + 2 * D_SWA_KV

BF16 = jnp.bfloat16
F32 = jnp.float32


def _params(n_grid_dims):
    return pltpu.CompilerParams(
        dimension_semantics=("arbitrary",) * n_grid_dims,
        vmem_limit_bytes=_VMEM_LIMIT_V7X,
    )


def _const_spec(shape):
    nd = len(shape)
    return pl.BlockSpec(shape, lambda *_: (0,) * nd, pipeline_mode=pl.Buffered(1))


def _rmsnorm_rows(x, g):
    ms = jnp.mean(x * x, axis=-1, keepdims=True)
    return x * lax.rsqrt(ms + EPS) * g


def _silu(x):
    return x * jax.nn.sigmoid(x)


def _proj_kernel(x_ref, g_ref, w_ref, o_ref, h_scr, *, row_chunk, sigmoid_out):
    tm = x_ref.shape[0]

    @pl.when(pl.program_id(1) == 0)
    def _():
        for r in range(0, tm, row_chunk):
            h_scr[r:r + row_chunk, :] = _rmsnorm_rows(
                x_ref[r:r + row_chunk, :], g_ref[...]).astype(BF16)

    acc = jnp.dot(h_scr[...], w_ref[...], preferred_element_type=F32)
    o_ref[...] = jax.nn.sigmoid(acc) if sigmoid_out else acc


def _proj(x2d, g, w, *, tm, tn, sigmoid_out, name):
    m, d = x2d.shape
    n = w.shape[1]
    kern = functools.partial(_proj_kernel, row_chunk=256, sigmoid_out=sigmoid_out)
    return pl.pallas_call(
        kern,
        out_shape=jax.ShapeDtypeStruct((m, n), F32),
        grid=(m // tm, n // tn),
        in_specs=[
            pl.BlockSpec((tm, d), lambda i, j: (i, 0)),
            _const_spec((1, d)),
            pl.BlockSpec((d, tn), lambda i, j: (0, j)),
        ],
        out_specs=pl.BlockSpec((tm, tn), lambda i, j: (i, j)),
        scratch_shapes=[pltpu.VMEM((tm, d), BF16)],
        compiler_params=_params(2),
        name=name,
    )(x2d, g, w)


def _rglru_kernel(xr_ref, g_ref, cw_ref, cb_ref, wax_ref, ba_ref, bx_ref, lam_ref, y_ref,
                  xpad_scr, a_scr, b_scr, h_scr, carry_scr):
    ts, c = xr_ref.shape
    groups = ts // _SUBLANES
    s = pl.program_id(1)

    @pl.when(s == 0)
    def _():
        xpad_scr[0:_SUBLANES, :] = jnp.zeros((_SUBLANES, c), F32)
        carry_scr[...] = jnp.zeros((_SUBLANES, c), F32)

    xpad_scr[_SUBLANES:_SUBLANES + ts, :] = xr_ref[...]
    conv = cb_ref[...] + sum(
        xpad_scr[_SUBLANES - k:_SUBLANES - k + ts, :] * cw_ref[k:k + 1, :] for k in range(CONV_W))
    xpad_scr[0:_SUBLANES, :] = xpad_scr[ts:ts + _SUBLANES, :]

    conv_bf = conv.astype(BF16)
    pre = [jnp.dot(conv_bf[:, n * RNN_BLOCK:(n + 1) * RNN_BLOCK], wax_ref[n],
                   preferred_element_type=F32) for n in range(RNN_BLOCKS)]
    pre_a = jnp.concatenate([p[:, :RNN_BLOCK] for p in pre], axis=1)
    pre_x = jnp.concatenate([p[:, RNN_BLOCK:] for p in pre], axis=1)
    gate_r = jax.nn.sigmoid(pre_a + ba_ref[...])
    gate_i = jax.nn.sigmoid(pre_x + bx_ref[...])

    neg_lam = -lam_ref[...]
    softplus = jnp.maximum(neg_lam, 0.0) + jnp.log1p(jnp.exp(-jnp.abs(neg_lam)))
    log_a = -LRU_C * gate_r * softplus
    a = jnp.exp(log_a)
    th = jnp.tanh(log_a)
    mult = jnp.sqrt(-2.0 * th / (1.0 - th))
    row = lax.broadcasted_iota(jnp.int32, (ts, c), 0)
    mult = jnp.where(row == jnp.where(s == 0, 0, -1), 1.0, mult)
    b = mult * gate_i * conv

    a3 = a.reshape(groups, _SUBLANES, c)
    b3 = b.reshape(groups, _SUBLANES, c)
    sub = lax.broadcasted_iota(jnp.int32, (groups, _SUBLANES, c), 1)
    d = 1
    while d < _SUBLANES:
        a_sh = pltpu.roll(a3, d, axis=1)
        b_sh = pltpu.roll(b3, d, axis=1)
        keep = sub >= d
        b3 = jnp.where(keep, a3 * b_sh + b3, b3)
        a3 = jnp.where(keep, a3 * a_sh, a3)
        d *= 2
    a_scr[...] = a3
    b_scr[...] = b3

    def body(gi, hprev):
        hg = a_scr[gi] * hprev + b_scr[gi]
        h_scr[gi] = hg
        return jnp.broadcast_to(hg[_SUBLANES - 1:_SUBLANES, :], (_SUBLANES, c))

    carry_scr[...] = lax.fori_loop(0, groups, body, carry_scr[...])

    h = h_scr[...].reshape(ts, c)
    y_ref[...] = (h * _silu(g_ref[...])).astype(BF16)


def _rglru(pm, conv_w, conv_b, wax, b_a, b_x, lam, *, batch, seq, ts):
    steps = seq // ts
    c = D_RNN
    groups = ts // _SUBLANES
    return pl.pallas_call(
        _rglru_kernel,
        out_shape=jax.ShapeDtypeStruct((batch * seq, c), BF16),
        grid=(batch, steps),
        in_specs=[
            pl.BlockSpec((ts, c), lambda b, s: (b * steps + s, _COL_XR)),
            pl.BlockSpec((ts, c), lambda b, s: (b * steps + s, _COL_GRG)),
            _const_spec((CONV_W, c)),
            _const_spec((1, c)),
            _const_spec((RNN_BLOCKS, RNN_BLOCK, 2 * RNN_BLOCK)),
            _const_spec((1, c)),
            _const_spec((1, c)),
            _const_spec((1, c)),
        ],
        out_specs=pl.BlockSpec((ts, c), lambda b, s: (b * steps + s, 0)),
        scratch_shapes=[
            pltpu.VMEM((ts + _SUBLANES, c), F32),
            pltpu.VMEM((groups, _SUBLANES, c), F32),
            pltpu.VMEM((groups, _SUBLANES, c), F32),
            pltpu.VMEM((groups, _SUBLANES, c), F32),
            pltpu.VMEM((_SUBLANES, c), F32),
        ],
        compiler_params=_params(2),
        name="rglru",
    )(pm, pm, conv_w, conv_b, wax, b_a, b_x, lam)


def _rel_bucket_table():
    qi = np.arange(WINDOW)[:, None]
    kj = np.arange(2 * WINDOW)[None, :]
    n = np.maximum(qi + WINDOW - kj, 0)
    max_exact = REL_BUCKETS // 2
    ratio = (np.log(np.maximum(n, 1).astype(np.float32) / np.float32(max_exact))
             / np.float32(math.log(REL_MAX_DIST / max_exact))).astype(np.float32)
    large = max_exact + (ratio * np.float32(REL_BUCKETS - max_exact)).astype(np.int32)
    large = np.minimum(large, REL_BUCKETS - 1)
    return np.where(n < max_exact, n, large).astype(np.int32)


def _bias_kernel(rb_ref, bucket_ref, o_ref):
    h = pl.program_id(0)
    bucket = bucket_ref[...]
    acc = jnp.zeros(bucket.shape, F32)
    for bkt in range(REL_BUCKETS):
        acc = jnp.where(bucket == bkt, rb_ref[bkt, h], acc)
    o_ref[0] = acc


def _bias_table(rel_bias):
    bucket = jnp.asarray(_rel_bucket_table())
    return pl.pallas_call(
        _bias_kernel,
        out_shape=jax.ShapeDtypeStruct((SWA_HEADS, WINDOW, 2 * WINDOW), F32),
        grid=(SWA_HEADS,),
        in_specs=[
            pl.BlockSpec(memory_space=pltpu.SMEM),
            _const_spec((WINDOW, 2 * WINDOW)),
        ],
        out_specs=pl.BlockSpec((1, WINDOW, 2 * WINDOW), lambda h: (h, 0, 0)),
        compiler_params=_params(1),
        name="swa_bias",
    )(rel_bias, bucket)


def _swa_kernel(sink_ref, q_ref, g_ref, kc_ref, kp_ref, vc_ref, vp_ref, bias_ref, y_ref):
    blk = WINDOW
    n = pl.program_id(1)
    half = SWA_HD

    kt = jnp.concatenate([kp_ref[...], kc_ref[...]], axis=0).T
    kt_sw = jnp.concatenate([kt[half:], kt[:half]], axis=0)
    row_lo = lax.broadcasted_iota(jnp.int32, kt.shape, 0) < half
    zero_k = jnp.zeros_like(kt)
    kt_even = (jnp.where(row_lo, kt, zero_k).astype(BF16), jnp.where(row_lo, kt_sw, zero_k).astype(BF16))
    kt_odd = (jnp.where(row_lo, zero_k, kt_sw).astype(BF16), jnp.where(row_lo, zero_k, kt).astype(BF16))

    v = jnp.concatenate([vp_ref[...], vc_ref[...]], axis=0)
    v_sw = pltpu.roll(v, half, axis=1)
    lane_lo = lax.broadcasted_iota(jnp.int32, v.shape, 1) < half
    zero_v = jnp.zeros_like(v)
    v_even = (jnp.where(lane_lo, v, zero_v).astype(BF16), jnp.where(lane_lo, v_sw, zero_v).astype(BF16))
    v_odd = (jnp.where(lane_lo, zero_v, v_sw).astype(BF16), jnp.where(lane_lo, zero_v, v).astype(BF16))

    qi = lax.broadcasted_iota(jnp.int32, (blk, 2 * blk), 0)
    kj = lax.broadcasted_iota(jnp.int32, (blk, 2 * blk), 1)
    first_key = jnp.where(n > 0, 0, blk)
    valid = (kj > qi) & (kj <= qi + blk) & (kj >= first_key)
    out_lane_lo = lax.broadcasted_iota(jnp.int32, (blk, 2 * half), 1) < half

    def head_probs(logits, head):
        sc = logits * (SWA_HD ** -0.5) + bias_ref[head]
        sc = jnp.where(valid, sc, NEG_INF)
        sink = sink_ref[head]
        m = jnp.maximum(jnp.max(sc, axis=-1, keepdims=True), sink)
        p = jnp.exp(sc - m)
        denom = jnp.sum(p, axis=-1, keepdims=True) + jnp.exp(sink - m)
        return p.astype(BF16), 1.0 / denom

    pairs_per_kv = SWA_HEADS // SWA_KV_HEADS // 2
    for kv in range(SWA_KV_HEADS):
        for j in range(pairs_per_kv):
            pair = kv * pairs_per_kv + j
            cols = slice(pair * 2 * half, (pair + 1) * 2 * half)
            qp = q_ref[:, cols].astype(BF16)
            p_e, inv_e = head_probs(jnp.dot(qp, kt_even[kv], preferred_element_type=F32), 2 * pair)
            p_o, inv_o = head_probs(jnp.dot(qp, kt_odd[kv], preferred_element_type=F32), 2 * pair + 1)
            out = (jnp.dot(p_e, v_even[kv], preferred_element_type=F32)
                   + jnp.dot(p_o, v_odd[kv], preferred_element_type=F32))
            out = out * jnp.where(out_lane_lo, inv_e, inv_o)
            y_ref[:, cols] = (out * _silu(g_ref[:, cols])).astype(BF16)


def _swa(pm, sinks, bias, *, batch, seq):
    nb = seq // WINDOW
    blk = WINDOW

    def cur(col):
        return lambda b, n: (b * nb + n, col)

    def prev(col):
        return lambda b, n: (b * nb + jnp.maximum(n - 1, 0), col)

    return pl.pallas_call(
        _swa_kernel,
        out_shape=jax.ShapeDtypeStruct((batch * seq, D_SWA), BF16),
        grid=(batch, nb),
        in_specs=[
            pl.BlockSpec(memory_space=pltpu.SMEM),
            pl.BlockSpec((blk, D_SWA), cur(_COL_QS)),
            pl.BlockSpec((blk, D_SWA), cur(_COL_GSWA)),
            pl.BlockSpec((blk, D_SWA_KV), cur(_COL_K)),
            pl.BlockSpec((blk, D_SWA_KV), prev(_COL_K)),
            pl.BlockSpec((blk, D_SWA_KV), cur(_COL_V)),
            pl.BlockSpec((blk, D_SWA_KV), prev(_COL_V)),
            _const_spec((SWA_HEADS, blk, 2 * blk)),
        ],
        out_specs=pl.BlockSpec((blk, D_SWA), lambda b, n: (b * nb + n, 0)),
        compiler_params=_params(2),
        name="swa",
    )(sinks, pm, pm, pm, pm, pm, pm, bias)


def _memkv_kernel(mem_ref, g_ref, w_ref, kt_ref, v_ref):
    memn = _rmsnorm_rows(mem_ref[...], g_ref[...]).astype(BF16)
    mkv = jnp.dot(memn, w_ref[...], preferred_element_type=F32)
    kt_ref[0] = mkv[:, :D_MEM].T.astype(BF16)
    v_ref[0] = mkv[:, D_MEM:].astype(BF16)


def _memkv(mem2d, g, w, *, batch, mem_len):
    return pl.pallas_call(
        _memkv_kernel,
        out_shape=(jax.ShapeDtypeStruct((batch, D_MEM, mem_len), BF16),
                   jax.ShapeDtypeStruct((batch, mem_len, D_MEM), BF16)),
        grid=(batch,),
        in_specs=[
            pl.BlockSpec((mem_len, D_MODEL), lambda b: (b, 0)),
            _const_spec((1, D_MODEL)),
            _const_spec((D_MODEL, 2 * D_MEM)),
        ],
        out_specs=(pl.BlockSpec((1, D_MEM, mem_len), lambda b: (b, 0, 0)),
                   pl.BlockSpec((1, mem_len, D_MEM), lambda b: (b, 0, 0))),
        compiler_params=_params(1),
        name="memkv",
    )(mem2d, g, w)


def _memattn_kernel(q_ref, g_ref, kt_ref, v_ref, y_ref):
    for h in range(MEM_HEADS):
        cols = slice(h * MEM_HD, (h + 1) * MEM_HD)
        qh = q_ref[:, cols].astype(BF16)
        sc = jnp.dot(qh, kt_ref[0, cols, :], preferred_element_type=F32) * (MEM_HD ** -0.5)
        m = jnp.max(sc, axis=-1, keepdims=True)
        p = jnp.exp(sc - m)
        inv = 1.0 / jnp.sum(p, axis=-1, keepdims=True)
        out = jnp.dot(p.astype(BF16), v_ref[0, :, cols], preferred_element_type=F32) * inv
        y_ref[:, cols] = (out * _silu(g_ref[:, cols])).astype(BF16)


def _memattn(pm, kt, v, *, batch, seq, ts):
    steps = seq // ts
    mem_len = v.shape[1]
    return pl.pallas_call(
        _memattn_kernel,
        out_shape=jax.ShapeDtypeStruct((batch * seq, D_MEM), BF16),
        grid=(batch, steps),
        in_specs=[
            pl.BlockSpec((ts, D_MEM), lambda b, s: (b * steps + s, _COL_QM)),
            pl.BlockSpec((ts, D_MEM), lambda b, s: (b * steps + s, _COL_GMEM)),
            pl.BlockSpec((1, D_MEM, mem_len), lambda b, s: (b, 0, 0)),
            pl.BlockSpec((1, mem_len, D_MEM), lambda b, s: (b, 0, 0)),
        ],
        out_specs=pl.BlockSpec((ts, D_MEM), lambda b, s: (b * steps + s, 0)),
        compiler_params=_params(2),
        name="memattn",
    )(pm, pm, kt, v)


def _merge_kernel(x_ref, yr_ref, ys_ref, ym_ref, pre_g_ref, wg0_ref, wg1_ref, wg2_ref, wbr_ref,
                  wout_ref, post_g_ref, o_ref, h_scr, *, row_chunk):
    tm = x_ref.shape[0]
    c = pl.program_id(1)

    @pl.when(c == 0)
    def _():
        for r in range(0, tm, row_chunk):
            h_scr[r:r + row_chunk, :] = _rmsnorm_rows(
                x_ref[r:r + row_chunk, :], pre_g_ref[...]).astype(BF16)
        o_ref[...] = jnp.zeros(o_ref.shape, F32)

    h = h_scr[...]
    merged = None
    for y_ref, wg_ref, b in ((yr_ref, wg0_ref, 0), (ys_ref, wg1_ref, 1), (ym_ref, wg2_ref, 2)):
        gate = jax.nn.sigmoid(jnp.dot(h, wg_ref[...], preferred_element_type=F32))
        term = gate * jnp.dot(y_ref[...], wbr_ref[b], preferred_element_type=F32)
        merged = term if merged is None else merged + term
    o_ref[...] += jnp.dot(merged.astype(BF16), wout_ref[...], preferred_element_type=F32)

    @pl.when(c == pl.num_programs(1) - 1)
    def _():
        for r in range(0, tm, row_chunk):
            rows = slice(r, r + row_chunk)
            o_ref[rows, :] = x_ref[rows, :] + _rmsnorm_rows(o_ref[rows, :], post_g_ref[...])


def _merge(x2d, y_rg, y_swa, y_mem, pre_g, w_gate, w_br, w_out, post_g, *, tm, dc):
    m, d = x2d.shape
    chunks = d // dc
    row = lambda i, c: (i, 0)
    const2 = lambda i, c: (0, 0)
    kern = functools.partial(_merge_kernel, row_chunk=256)
    return pl.pallas_call(
        kern,
        out_shape=jax.ShapeDtypeStruct((m, d), F32),
        grid=(m // tm, chunks),
        in_specs=[
            pl.BlockSpec((tm, d), row),
            pl.BlockSpec((tm, D_RNN), row),
            pl.BlockSpec((tm, D_SWA), row),
            pl.BlockSpec((tm, D_MEM), row),
            pl.BlockSpec((1, d), const2),
            pl.BlockSpec((d, dc), lambda i, c: (0, c)),
            pl.BlockSpec((d, dc), lambda i, c: (0, chunks + c)),
            pl.BlockSpec((d, dc), lambda i, c: (0, 2 * chunks + c)),
            pl.BlockSpec((N_BRANCH, D_RNN, dc), lambda i, c: (0, 0, c)),
            pl.BlockSpec((dc, d), lambda i, c: (c, 0)),
            pl.BlockSpec((1, d), const2),
        ],
        out_specs=pl.BlockSpec((tm, d), row),
        scratch_shapes=[pltpu.VMEM((tm, d), BF16)],
        compiler_params=_params(2),
        name="merge",
    )(x2d, y_rg, y_swa, y_mem, pre_g, w_gate, w_gate, w_gate, w_br, w_out, post_g)


def _layer(x2d, mem2d, pre_g, post_g, mem_g, w_in, conv_w, conv_b, w_a, b_a, w_x, b_x, lam, sinks,
           bias, w_mem_kv, w_br_rg, w_br_swa, w_br_mem, w_out, *, batch, seq, mem_len):
    o = np.cumsum((D_RNN, D_RNN, D_SWA, D_SWA_KV, D_SWA_KV, D_SWA, D_MEM, D_MEM))
    cols = lambda lo, hi: w_in[:, lo:hi]
    w_main = jnp.concatenate(
        [cols(0, o[0]), cols(o[0], o[1]), cols(o[1], o[2]), cols(o[4], o[5]), cols(o[5], o[6]),
         cols(o[6], o[7]), cols(o[2], o[3]), cols(o[3], o[4])], axis=1).astype(BF16)
    w_gate = w_in[:, o[7]:].astype(BF16)
    row = lambda t: t.reshape(1, -1)

    pm = _proj(x2d, row(pre_g), w_main, tm=1024, tn=1280, sigmoid_out=False, name="proj_main")

    wax = jnp.concatenate([w_a, w_x], axis=-1).astype(BF16)
    y_rg = _rglru(pm, conv_w, row(conv_b), wax, row(b_a), row(b_x), row(lam),
                  batch=batch, seq=seq, ts=512)
    y_swa = _swa(pm, sinks, bias, batch=batch, seq=seq)
    kt, v = _memkv(mem2d, row(mem_g), w_mem_kv.astype(BF16), batch=batch, mem_len=mem_len)
    y_mem = _memattn(pm, kt, v, batch=batch, seq=seq, ts=512)

    w_br = jnp.stack([w_br_rg, w_br_swa, w_br_mem]).astype(BF16)
    return _merge(x2d, y_rg, y_swa, y_mem, row(pre_g), w_gate, w_br, w_out.astype(BF16), row(post_g),
                  tm=512, dc=512)


def kernel(x, mem, pre_norm_g, post_norm_g, mem_norm_g, w_in, conv_w, conv_b, w_rg_a, b_rg_a, w_rg_x,
           b_rg_x, lru_lambda, swa_sinks, rel_bias, w_mem_kv, w_br_rg, w_br_swa, w_br_mem, w_out):
    batch, seq, d = x.shape
    mem_len = mem.shape[1]
    depth = w_in.shape[0]
    bias = _bias_table(rel_bias)
    x2d = x.reshape(batch * seq, d)
    mem2d = mem.reshape(batch * mem_len, d)
    for l in range(depth):
        x2d = _layer(x2d, mem2d, pre_norm_g[l], post_norm_g[l], mem_norm_g[l], w_in[l], conv_w[l],
                     conv_b[l], w_rg_a[l], b_rg_a[l], w_rg_x[l], b_rg_x[l], lru_lambda[l],
                     swa_sinks[l], bias, w_mem_kv[l], w_br_rg[l], w_br_swa[l], w_br_mem[l], w_out[l],
                     batch=batch, seq=seq, mem_len=mem_len)
    return x2d.reshape(batch, seq, d)
```

```python
import functools
import math

import numpy as np
import jax
import jax.numpy as jnp
from jax import lax
from jax.experimental import pallas as pl
from jax.experimental.pallas import tpu as pltpu

D_MODEL = 2048
D_RNN = 1024
RNN_BLOCKS = 8
RNN_BLOCK = D_RNN // RNN_BLOCKS
CONV_W = 4
LRU_C = 8.0
SWA_HEADS = 16
SWA_KV_HEADS = 2
SWA_HD = 64
D_SWA = SWA_HEADS * SWA_HD
D_SWA_KV = SWA_KV_HEADS * SWA_HD
WINDOW = 128
MEM_HEADS = 4
MEM_HD = 256
D_MEM = MEM_HEADS * MEM_HD
REL_BUCKETS = 32
REL_MAX_DIST = 128
N_BRANCH = 3
EPS = 1e-6
NEG_INF = -1e30

_VMEM_LIMIT_V7X = 56 * 1024 * 1024
_SUBLANES = 8

_BRANCH_ROWS = 512
_MERGE_ROWS = 512
_MERGE_CHUNK = 512

BF16 = jnp.bfloat16
F32 = jnp.float32


def _params(n_grid_dims):
    return pltpu.CompilerParams(
        dimension_semantics=("arbitrary",) * n_grid_dims,
        vmem_limit_bytes=_VMEM_LIMIT_V7X,
    )


def _const_spec(shape):
    nd = len(shape)
    return pl.BlockSpec(shape, lambda *_: (0,) * nd, pipeline_mode=pl.Buffered(1))


def _rmsnorm_rows(x, g):
    ms = jnp.mean(x * x, axis=-1, keepdims=True)
    return x * lax.rsqrt(ms + EPS) * g


def _silu(x):
    return x * jax.nn.sigmoid(x)


def _dot(a, b):
    return jnp.dot(a, b, preferred_element_type=F32)


def _prenorm_kernel(x_ref, g_ref, h_ref):
    h_ref[...] = _rmsnorm_rows(x_ref[...], g_ref[...]).astype(BF16)


def _prenorm(x2d, g, *, tm):
    m, d = x2d.shape
    return pl.pallas_call(
        _prenorm_kernel,
        out_shape=jax.ShapeDtypeStruct((m, d), BF16),
        grid=(m // tm,),
        in_specs=[pl.BlockSpec((tm, d), lambda i: (i, 0)), _const_spec((1, d))],
        out_specs=pl.BlockSpec((tm, d), lambda i: (i, 0)),
        compiler_params=_params(1),
        name="prenorm",
    )(x2d, g)


def _rglru_kernel(h_ref, w_ref, cw_ref, cb_ref, wax_ref, ba_ref, bx_ref, lam_ref, y_ref,
                  xpad_scr, carry_scr):
    ts = h_ref.shape[0]
    c = D_RNN
    groups = ts // _SUBLANES
    s = pl.program_id(1)

    @pl.when(s == 0)
    def _():
        xpad_scr[0:_SUBLANES, :] = jnp.zeros((_SUBLANES, c), F32)
        carry_scr[...] = jnp.zeros((1, c), F32)

    h_in = h_ref[...]
    xr = _dot(h_in, w_ref[:, :c])
    gate_silu = _silu(_dot(h_in, w_ref[:, c:]))

    xpad_scr[_SUBLANES:_SUBLANES + ts, :] = xr
    conv = cb_ref[...] + sum(
        xpad_scr[_SUBLANES - k:_SUBLANES - k + ts, :] * cw_ref[k:k + 1, :] for k in range(CONV_W))
    xpad_scr[0:_SUBLANES, :] = xpad_scr[ts:ts + _SUBLANES, :]

    conv_bf = conv.astype(BF16)
    pre = [_dot(conv_bf[:, n * RNN_BLOCK:(n + 1) * RNN_BLOCK], wax_ref[n]) for n in range(RNN_BLOCKS)]
    pre_a = jnp.concatenate([p[:, :RNN_BLOCK] for p in pre], axis=1)
    pre_x = jnp.concatenate([p[:, RNN_BLOCK:] for p in pre], axis=1)
    gate_r = jax.nn.sigmoid(pre_a + ba_ref[...])
    gate_i = jax.nn.sigmoid(pre_x + bx_ref[...])

    neg_lam = -lam_ref[...]
    softplus = jnp.maximum(neg_lam, 0.0) + jnp.log1p(jnp.exp(-jnp.abs(neg_lam)))
    log_a = -LRU_C * gate_r * softplus
    a = jnp.exp(log_a)
    th = jnp.tanh(log_a)
    mult = jnp.sqrt(-2.0 * th / (1.0 - th))
    row = lax.broadcasted_iota(jnp.int32, (ts, c), 0)
    mult = jnp.where(row == jnp.where(s == 0, 0, -1), 1.0, mult)
    b = mult * gate_i * conv

    a3 = a.reshape(groups, _SUBLANES, c)
    b3 = b.reshape(groups, _SUBLANES, c)
    sub = lax.broadcasted_iota(jnp.int32, (groups, _SUBLANES, c), 1)
    d = 1
    while d < _SUBLANES:
        a_sh = pltpu.roll(a3, d, axis=1)
        b_sh = pltpu.roll(b3, d, axis=1)
        keep = sub >= d
        b3 = jnp.where(keep, a3 * b_sh + b3, b3)
        a3 = jnp.where(keep, a3 * a_sh, a3)
        d *= 2

    last = _SUBLANES - 1
    state = carry_scr[...]
    h_groups = []
    for gi in range(groups):
        h_groups.append(a3[gi] * state + b3[gi])
        state = a3[gi, last:, :] * state + b3[gi, last:, :]
    carry_scr[...] = state

    y_ref[...] = (jnp.concatenate(h_groups, axis=0) * gate_silu).astype(BF16)


def _rglru(h, w_rg, conv_w, conv_b, wax, b_a, b_x, lam, *, batch, seq, ts):
    steps = seq // ts
    c = D_RNN
    d = h.shape[1]
    return pl.pallas_call(
        _rglru_kernel,
        out_shape=jax.ShapeDtypeStruct((batch * seq, c), BF16),
        grid=(batch, steps),
        in_specs=[
            pl.BlockSpec((ts, d), lambda b, s: (b * steps + s, 0)),
            _const_spec((d, 2 * c)),
            _const_spec((CONV_W, c)),
            _const_spec((1, c)),
            _const_spec((RNN_BLOCKS, RNN_BLOCK, 2 * RNN_BLOCK)),
            _const_spec((1, c)),
            _const_spec((1, c)),
            _const_spec((1, c)),
        ],
        out_specs=pl.BlockSpec((ts, c), lambda b, s: (b * steps + s, 0)),
        scratch_shapes=[
            pltpu.VMEM((ts + _SUBLANES, c), F32),
            pltpu.VMEM((1, c), F32),
        ],
        compiler_params=_params(2),
        name="rglru",
    )(h, w_rg, conv_w, conv_b, wax, b_a, b_x, lam)


def _rel_bucket_table():
    qi = np.arange(WINDOW)[:, None]
    kj = np.arange(2 * WINDOW)[None, :]
    n = np.maximum(qi + WINDOW - kj, 0)
    max_exact = REL_BUCKETS // 2
    ratio = (np.log(np.maximum(n, 1).astype(np.float32) / np.float32(max_exact))
             / np.float32(math.log(REL_MAX_DIST / max_exact))).astype(np.float32)
    large = max_exact + (ratio * np.float32(REL_BUCKETS - max_exact)).astype(np.int32)
    large = np.minimum(large, REL_BUCKETS - 1)
    return np.where(n < max_exact, n, large).astype(np.int32)


def _bias_kernel(rb_ref, bucket_ref, o_ref):
    variant = pl.program_id(0)
    h = pl.program_id(1)
    bucket = bucket_ref[...]
    acc = jnp.zeros(bucket.shape, F32)
    for bkt in range(REL_BUCKETS):
        acc = jnp.where(bucket == bkt, rb_ref[bkt, h], acc)
    qi = lax.broadcasted_iota(jnp.int32, bucket.shape, 0)
    kj = lax.broadcasted_iota(jnp.int32, bucket.shape, 1)
    first_key = jnp.where(variant == 1, WINDOW, 0)
    acc = jnp.where(kj > qi, acc, NEG_INF)
    acc = jnp.where(kj <= qi + WINDOW, acc, NEG_INF)
    o_ref[0, 0] = jnp.where(kj >= first_key, acc, NEG_INF)


def _bias_table(rel_bias):
    bucket = jnp.asarray(_rel_bucket_table())
    return pl.pallas_call(
        _bias_kernel,
        out_shape=jax.ShapeDtypeStruct((2, SWA_HEADS, WINDOW, 2 * WINDOW), F32),
        grid=(2, SWA_HEADS),
        in_specs=[
            pl.BlockSpec(memory_space=pltpu.SMEM),
            _const_spec((WINDOW, 2 * WINDOW)),
        ],
        out_specs=pl.BlockSpec((1, 1, WINDOW, 2 * WINDOW), lambda v, h: (v, h, 0, 0)),
        compiler_params=_params(2),
        name="swa_bias",
    )(rel_bias, bucket)


def _swa_kernel(sink_ref, h_ref, w_ref, bias_ref, y_ref, kprev_scr, vprev_scr):
    ts = h_ref.shape[0]
    blk = WINDOW
    half = SWA_HD
    pairs_per_kv = SWA_HEADS // SWA_KV_HEADS // 2
    s = pl.program_id(1)

    @pl.when(s == 0)
    def _():
        kprev_scr[...] = jnp.zeros(kprev_scr.shape, F32)
        vprev_scr[...] = jnp.zeros(vprev_scr.shape, F32)

    h_in = h_ref[...]
    q = (_dot(h_in, w_ref[:, :D_SWA]) * (SWA_HD ** -0.5)).astype(BF16)
    gate_silu = _silu(_dot(h_in, w_ref[:, D_SWA:2 * D_SWA]))
    kv = _dot(h_in, w_ref[:, 2 * D_SWA:])
    k = kv[:, :D_SWA_KV]
    v = kv[:, D_SWA_KV:]
    k_ext = jnp.concatenate([kprev_scr[...], k], axis=0)
    v_ext = jnp.concatenate([vprev_scr[...], v], axis=0)
    kprev_scr[...] = k[ts - blk:, :]
    vprev_scr[...] = v[ts - blk:, :]

    kt = k_ext.T
    kt_sw = jnp.concatenate([kt[half:], kt[:half]], axis=0)
    row_lo = lax.broadcasted_iota(jnp.int32, kt.shape, 0) < half
    zero_k = jnp.zeros_like(kt)
    kt_par = ((jnp.where(row_lo, kt, zero_k).astype(BF16), jnp.where(row_lo, kt_sw, zero_k).astype(BF16)),
              (jnp.where(row_lo, zero_k, kt_sw).astype(BF16), jnp.where(row_lo, zero_k, kt).astype(BF16)))
    v_sw = pltpu.roll(v_ext, half, axis=1)
    lane_lo = lax.broadcasted_iota(jnp.int32, v_ext.shape, 1) < half
    zero_v = jnp.zeros_like(v_ext)
    v_par = ((jnp.where(lane_lo, v_ext, zero_v).astype(BF16), jnp.where(lane_lo, v_sw, zero_v).astype(BF16)),
             (jnp.where(lane_lo, zero_v, v_sw).astype(BF16), jnp.where(lane_lo, zero_v, v_ext).astype(BF16)))
    out_lane_lo = lax.broadcasted_iota(jnp.int32, (blk, 2 * half), 1) < half

    for j in range(ts // blk):
        rows = slice(j * blk, (j + 1) * blk)
        keys = slice(j * blk, (j + 2) * blk)
        variant = jnp.where(s == 0, 1, 0) if j == 0 else 0
        for kvh in range(SWA_KV_HEADS):
            pair0 = kvh * pairs_per_kv
            qs = jnp.concatenate(
                [q[rows, (pair0 + i) * 2 * half:(pair0 + i + 1) * 2 * half] for i in range(pairs_per_kv)],
                axis=0)
            o_par, inv_par = [], []
            for parity in range(2):
                sc_all = _dot(qs, kt_par[parity][kvh][:, keys])
                probs, invs = [], []
                for i in range(pairs_per_kv):
                    head = 2 * (pair0 + i) + parity
                    sc = sc_all[i * blk:(i + 1) * blk, :] + bias_ref[variant, head]
                    sink = sink_ref[head]
                    m = jnp.maximum(jnp.max(sc, axis=-1, keepdims=True), sink)
                    p = jnp.exp(sc - m)
                    denom = jnp.sum(p, axis=-1, keepdims=True) + jnp.exp(sink - m)
                    probs.append(p.astype(BF16))
                    invs.append(1.0 / denom)
                o_par.append(_dot(jnp.concatenate(probs, axis=0), v_par[parity][kvh][keys, :]))
                inv_par.append(invs)
            for i in range(pairs_per_kv):
                cols = slice((pair0 + i) * 2 * half, (pair0 + i + 1) * 2 * half)
                prow = slice(i * blk, (i + 1) * blk)
                out = (o_par[0][prow, :] + o_par[1][prow, :]) * jnp.where(
                    out_lane_lo, inv_par[0][i], inv_par[1][i])
                y_ref[rows, cols] = (out * gate_silu[rows, cols]).astype(BF16)


def _swa(h, w_swa, sinks, bias, *, batch, seq, ts):
    steps = seq // ts
    d = h.shape[1]
    return pl.pallas_call(
        _swa_kernel,
        out_shape=jax.ShapeDtypeStruct((batch * seq, D_SWA), BF16),
        grid=(batch, steps),
        in_specs=[
            pl.BlockSpec(memory_space=pltpu.SMEM),
            pl.BlockSpec((ts, d), lambda b, s: (b * steps + s, 0)),
            _const_spec((d, 2 * D_SWA + 2 * D_SWA_KV)),
            _const_spec((2, SWA_HEADS, WINDOW, 2 * WINDOW)),
        ],
        out_specs=pl.BlockSpec((ts, D_SWA), lambda b, s: (b * steps + s, 0)),
        scratch_shapes=[
            pltpu.VMEM((WINDOW, D_SWA_KV), F32),
            pltpu.VMEM((WINDOW, D_SWA_KV), F32),
        ],
        compiler_params=_params(2),
        name="swa",
    )(sinks, h, w_swa, bias)


def _memkv_kernel(mem_ref, g_ref, w_ref, kt_ref, v_ref):
    memn = _rmsnorm_rows(mem_ref[...], g_ref[...]).astype(BF16)
    mkv = _dot(memn, w_ref[...])
    kt_ref[0] = mkv[:, :D_MEM].T.astype(BF16)
    v_ref[0] = mkv[:, D_MEM:].astype(BF16)


def _memkv(mem2d, g, w, *, batch, mem_len):
    return pl.pallas_call(
        _memkv_kernel,
        out_shape=(jax.ShapeDtypeStruct((batch, D_MEM, mem_len), BF16),
                   jax.ShapeDtypeStruct((batch, mem_len, D_MEM), BF16)),
        grid=(batch,),
        in_specs=[
            pl.BlockSpec((mem_len, D_MODEL), lambda b: (b, 0)),
            _const_spec((1, D_MODEL)),
            _const_spec((D_MODEL, 2 * D_MEM)),
        ],
        out_specs=(pl.BlockSpec((1, D_MEM, mem_len), lambda b: (b, 0, 0)),
                   pl.BlockSpec((1, mem_len, D_MEM), lambda b: (b, 0, 0))),
        compiler_params=_params(1),
        name="memkv",
    )(mem2d, g, w)


def _memattn_kernel(h_ref, w_ref, kt_ref, v_ref, y_ref):
    h_in = h_ref[...]
    q = (_dot(h_in, w_ref[:, :D_MEM]) * (MEM_HD ** -0.5)).astype(BF16)
    gate_silu = _silu(_dot(h_in, w_ref[:, D_MEM:]))
    for hd in range(MEM_HEADS):
        cols = slice(hd * MEM_HD, (hd + 1) * MEM_HD)
        sc = _dot(q[:, cols], kt_ref[0, cols, :])
        m = jnp.max(sc, axis=-1, keepdims=True)
        p = jnp.exp(sc - m)
        inv = 1.0 / jnp.sum(p, axis=-1, keepdims=True)
        out = _dot(p.astype(BF16), v_ref[0, :, cols]) * inv
        y_ref[:, cols] = (out * gate_silu[:, cols]).astype(BF16)


def _memattn(h, w_mem, kt, v, *, batch, seq, ts):
    steps = seq // ts
    d = h.shape[1]
    mem_len = v.shape[1]
    return pl.pallas_call(
        _memattn_kernel,
        out_shape=jax.ShapeDtypeStruct((batch * seq, D_MEM), BF16),
        grid=(batch, steps),
        in_specs=[
            pl.BlockSpec((ts, d), lambda b, s: (b * steps + s, 0)),
            _const_spec((d, 2 * D_MEM)),
            pl.BlockSpec((1, D_MEM, mem_len), lambda b, s: (b, 0, 0)),
            pl.BlockSpec((1, mem_len, D_MEM), lambda b, s: (b, 0, 0)),
        ],
        out_specs=pl.BlockSpec((ts, D_MEM), lambda b, s: (b * steps + s, 0)),
        compiler_params=_params(2),
        name="memattn",
    )(h, w_mem, kt, v)


def _merge_kernel(x_ref, h_ref, yr_ref, ys_ref, ym_ref, wg0_ref, wg1_ref, wg2_ref, wbr_ref,
                  wout_ref, post_g_ref, o_ref, *, row_chunk):
    tm = x_ref.shape[0]
    c = pl.program_id(1)

    @pl.when(c == 0)
    def _():
        o_ref[...] = jnp.zeros(o_ref.shape, F32)

    h_in = h_ref[...]
    merged = None
    for y_ref, wg_ref, b in ((yr_ref, wg0_ref, 0), (ys_ref, wg1_ref, 1), (ym_ref, wg2_ref, 2)):
        term = jax.nn.sigmoid(_dot(h_in, wg_ref[...])) * _dot(y_ref[...], wbr_ref[b])
        merged = term if merged is None else merged + term
    o_ref[...] += _dot(merged.astype(BF16), wout_ref[...])

    @pl.when(c == pl.num_programs(1) - 1)
    def _():
        for r in range(0, tm, row_chunk):
            rows = slice(r, r + row_chunk)
            o_ref[rows, :] = x_ref[rows, :] + _rmsnorm_rows(o_ref[rows, :], post_g_ref[...])


def _merge(x2d, h, y_rg, y_swa, y_mem, w_gate, w_br, w_out, post_g, *, tm, dc):
    m, d = x2d.shape
    chunks = d // dc
    row = lambda i, c: (i, 0)
    kern = functools.partial(_merge_kernel, row_chunk=256)
    return pl.pallas_call(
        kern,
        out_shape=jax.ShapeDtypeStruct((m, d), F32),
        grid=(m // tm, chunks),
        in_specs=[
            pl.BlockSpec((tm, d), row),
            pl.BlockSpec((tm, d), row),
            pl.BlockSpec((tm, D_RNN), row),
            pl.BlockSpec((tm, D_SWA), row),
            pl.BlockSpec((tm, D_MEM), row),
            pl.BlockSpec((d, dc), lambda i, c: (0, c)),
            pl.BlockSpec((d, dc), lambda i, c: (0, chunks + c)),
            pl.BlockSpec((d, dc), lambda i, c: (0, 2 * chunks + c)),
            pl.BlockSpec((N_BRANCH, D_RNN, dc), lambda i, c: (0, 0, c)),
            pl.BlockSpec((dc, d), lambda i, c: (c, 0)),
            pl.BlockSpec((1, d), lambda i, c: (0, 0)),
        ],
        out_specs=pl.BlockSpec((tm, d), row),
        compiler_params=_params(2),
        name="merge",
    )(x2d, h, y_rg, y_swa, y_mem, w_gate, w_gate, w_gate, w_br, w_out, post_g)


def _layer(x2d, mem2d, pre_g, post_g, mem_g, w_in, conv_w, conv_b, w_a, b_a, w_x, b_x, lam, sinks,
           bias, w_mem_kv, w_br_rg, w_br_swa, w_br_mem, w_out, *, batch, seq, mem_len):
    o = np.cumsum((D_RNN, D_RNN, D_SWA, D_SWA_KV, D_SWA_KV, D_SWA, D_MEM, D_MEM))
    w_rg = w_in[:, :o[1]].astype(BF16)
    w_swa = jnp.concatenate([w_in[:, o[1]:o[2]], w_in[:, o[4]:o[5]], w_in[:, o[2]:o[4]]],
                            axis=1).astype(BF16)
    w_mem = w_in[:, o[5]:o[7]].astype(BF16)
    w_gate = w_in[:, o[7]:].astype(BF16)
    row = lambda t: t.reshape(1, -1)
    ts = _BRANCH_ROWS

    h = _prenorm(x2d, row(pre_g), tm=ts)
    wax = jnp.concatenate([w_a, w_x], axis=-1).astype(BF16)
    y_rg = _rglru(h, w_rg, conv_w, row(conv_b), wax, row(b_a), row(b_x), row(lam),
                  batch=batch, seq=seq, ts=ts)
    y_swa = _swa(h, w_swa, sinks, bias, batch=batch, seq=seq, ts=ts)
    kt, v = _memkv(mem2d, row(mem_g), w_mem_kv.astype(BF16), batch=batch, mem_len=mem_len)
    y_mem = _memattn(h, w_mem, kt, v, batch=batch, seq=seq, ts=ts)

    w_br = jnp.stack([w_br_rg, w_br_swa, w_br_mem]).astype(BF16)
    return _merge(x2d, h, y_rg, y_swa, y_mem, w_gate, w_br, w_out.astype(BF16), row(post_g),
                  tm=_MERGE_ROWS, dc=_MERGE_CHUNK)


def kernel(x, mem, pre_norm_g, post_norm_g, mem_norm_g, w_in, conv_w, conv_b, w_rg_a, b_rg_a, w_rg_x,
           b_rg_x, lru_lambda, swa_sinks, rel_bias, w_mem_kv, w_br_rg, w_br_swa, w_br_mem, w_out):
    batch, seq, d = x.shape
    mem_len = mem.shape[1]
    depth = w_in.shape[0]
    bias = _bias_table(rel_bias)
    x2d = x.reshape(batch * seq, d)
    mem2d = mem.reshape(batch * mem_len, d)
    for l in range(depth):
        x2d = _layer(x2d, mem2d, pre_norm_g[l], post_norm_g[l], mem_norm_g[l], w_in[l], conv_w[l],
                     conv_b[l], w_rg_a[l], b_rg_a[l], w_rg_x[l], b_rg_x[l], lru_lambda[l],
                     swa_sinks[l], bias, w_mem_kv[l], w_br_rg[l], w_br_swa[l], w_br_mem[l], w_out[l],
                     batch=batch, seq=seq, mem_len=mem_len)
    return x2d.reshape(batch, seq, d)
```
